```python
import math
import jax, jax.numpy as jnp
from jax import lax
import numpy as np

D_MODEL = 1024
BATCH = 8
SEQ = 4096
DEPTH = 4

N_MIXERS = 3
N_RET = len(range(0, DEPTH, N_MIXERS))
N_CF = len(range(1, DEPTH, N_MIXERS))
N_SC = len(range(2, DEPTH, N_MIXERS))
RET_HEADS = 4
RET_DK = D_MODEL // RET_HEADS
RET_VDIM = 2 * D_MODEL
RET_DV = RET_VDIM // RET_HEADS
RET_CHUNK = 128
RET_IN = 2 * D_MODEL + 2 * RET_VDIM
ROPE_BASE = 10000.0
CF_KERNEL = 31
SC_KERNEL = 3
D_FF = 4 * D_MODEL
RMS_EPS = 1e-6
LN_EPS = 1e-5

kernel_name = "hybrid_retention_conformer_shortconv_trunk"

F32 = jnp.float32


def rms_norm(x, g):
    xf = x.astype(F32)
    y = xf * lax.rsqrt(jnp.mean(xf * xf, axis=-1, keepdims=True) + RMS_EPS)
    return (y * g.astype(F32)).astype(x.dtype)


def causal_depthwise_conv(x, w):
    k = w.shape[0]
    return lax.conv_general_dilated(
        x, w[:, None, :].astype(x.dtype), window_strides=(1,),
        padding=[(k - 1, 0)], dimension_numbers=("NWC", "WIO", "NWC"),
        feature_group_count=x.shape[-1])


def apply_rotary(t, cos, sin):
    t1, t2 = jnp.split(t, 2, axis=-1)
    return jnp.concatenate([t1 * cos - t2 * sin, t2 * cos + t1 * sin], axis=-1)


def chunkwise_retention(q, k, v):
    bsz, s, h, dk = q.shape
    dv = v.shape[-1]
    nc = s // RET_CHUNK
    log_gamma = jnp.log1p(-jnp.exp2(-5.0 - jnp.arange(h, dtype=F32)))
    idx = jnp.arange(RET_CHUNK, dtype=F32)
    rel = idx[:, None] - idx[None, :]
    intra = jnp.where(rel >= 0, jnp.exp(log_gamma[:, None, None] * jnp.maximum(rel, 0.0)), 0.0)
    q_decay = jnp.exp(log_gamma[:, None] * (idx + 1.0))
    k_decay = jnp.exp(log_gamma[:, None] * (RET_CHUNK - 1.0 - idx))
    chunk_decay = jnp.exp(log_gamma * RET_CHUNK)

    def to_chunks(t):
        return t.reshape(bsz, nc, RET_CHUNK, h, t.shape[-1]).transpose(1, 0, 3, 2, 4)

    def step(state, qkv):
        qc, kc, vc = qkv
        scores = jnp.einsum("bhid,bhjd->bhij", qc, kc) * intra
        out = (jnp.einsum("bhij,bhjv->bhiv", scores, vc)
               + jnp.einsum("bhid,bhdv->bhiv", qc * q_decay[..., None], state))
        state = (state * chunk_decay[:, None, None]
                 + jnp.einsum("bhjd,bhjv->bhdv", kc * k_decay[..., None], vc))
        return state, out

    state0 = jnp.zeros((bsz, h, dk, dv), F32)
    _, ys = lax.scan(step, state0, (to_chunks(q), to_chunks(k), to_chunks(v)))
    return ys.transpose(1, 0, 3, 2, 4).reshape(bsz, s, h, dv)


def retention_mixer(x, positions, w_in, gn_g, w_out):
    bsz, s, _ = x.shape
    proj = x @ w_in
    q, k, v, g = jnp.split(proj, [D_MODEL, 2 * D_MODEL, 2 * D_MODEL + RET_VDIM], axis=-1)
    q = q.reshape(bsz, s, RET_HEADS, RET_DK).astype(F32)
    k = k.reshape(bsz, s, RET_HEADS, RET_DK).astype(F32)
    v = v.reshape(bsz, s, RET_HEADS, RET_DV).astype(F32)
    inv_freq = ROPE_BASE ** (-jnp.arange(0, RET_DK, 2, dtype=F32) / RET_DK)
    ang = positions.astype(F32)[..., None, None] * inv_freq
    cos, sin = jnp.cos(ang), jnp.sin(ang)
    q = apply_rotary(q, cos, sin)
    k = apply_rotary(k, cos, sin) * (RET_DK ** -0.5)
    y = chunkwise_retention(q, k, v)
    mean = jnp.mean(y, axis=-1, keepdims=True)
    var = jnp.mean(jnp.square(y - mean), axis=-1, keepdims=True)
    y = ((y - mean) * lax.rsqrt(var + LN_EPS)).reshape(bsz, s, RET_VDIM) * gn_g.astype(F32)
    y = (jax.nn.silu(g.astype(F32)) * y).astype(x.dtype)
    return y @ w_out


def conformer_conv_mixer(x, w_pw1, b_pw1, w_dw, b_dw, ln_g, ln_b, w_pw2, b_pw2):
    h = x @ w_pw1 + b_pw1
    a, gate = jnp.split(h, 2, axis=-1)
    h = a * jax.nn.sigmoid(gate)
    h = causal_depthwise_conv(h, w_dw) + b_dw
    hf = h.astype(F32)
    mean = jnp.mean(hf, axis=-1, keepdims=True)
    var = jnp.mean(jnp.square(hf - mean), axis=-1, keepdims=True)
    hf = (hf - mean) * lax.rsqrt(var + LN_EPS) * ln_g.astype(F32) + ln_b.astype(F32)
    h = jax.nn.silu(hf).astype(x.dtype)
    return h @ w_pw2 + b_pw2


def short_conv_mixer(x, w_in, w_conv, w_out):
    gb, gc, h = jnp.split(x @ w_in, 3, axis=-1)
    u = causal_depthwise_conv(gc * h, w_conv)
    return (gb * u) @ w_out


def squared_relu_mlp(x, w1, w2):
    return jnp.square(jax.nn.relu(x @ w1)) @ w2


def setup_inputs(seed: int = 0) -> dict:
    key = jax.random.key(seed)
    ks = iter(jax.random.split(key, 32))

    def dense(shape, fan_in):
        return jax.random.normal(next(ks), shape, F32) * (fan_in ** -0.5)

    def small(shape, scale):
        return jax.random.normal(next(ks), shape, F32) * scale

    x = jax.random.normal(next(ks), (BATCH, SEQ, D_MODEL), F32)
    positions = jnp.broadcast_to(jnp.arange(SEQ, dtype=jnp.int32), (BATCH, SEQ))
    return {
        "x": x,
        "positions": positions,
        "norm_g": 1.0 + small((DEPTH, 4, D_MODEL), 0.05),
        "ret_w_in": dense((N_RET, D_MODEL, RET_IN), D_MODEL),
        "ret_gn_g": 1.0 + small((N_RET, RET_VDIM), 0.05),
        "ret_w_out": dense((N_RET, RET_VDIM, D_MODEL), RET_VDIM),
        "cf_w_pw1": dense((N_CF, D_MODEL, 2 * D_MODEL), D_MODEL),
        "cf_b_pw1": small((N_CF, 2 * D_MODEL), 0.01),
        "cf_w_dw": dense((N_CF, CF_KERNEL, D_MODEL), CF_KERNEL),
        "cf_b_dw": small((N_CF, D_MODEL), 0.01),
        "cf_ln_g": 1.0 + small((N_CF, D_MODEL), 0.05),
        "cf_ln_b": small((N_CF, D_MODEL), 0.01),
        "cf_w_pw2": dense((N_CF, D_MODEL, D_MODEL), D_MODEL),
        "cf_b_pw2": small((N_CF, D_MODEL), 0.01),
        "sc_w_in": dense((N_SC, D_MODEL, 3 * D_MODEL), D_MODEL),
        "sc_w_conv": dense((N_SC, SC_KERNEL, D_MODEL), SC_KERNEL),
        "sc_w_out": dense((N_SC, D_MODEL, D_MODEL), D_MODEL),
        "mlp_w1": dense((DEPTH, D_MODEL, D_FF), D_MODEL),
        "mlp_w2": dense((DEPTH, D_FF, D_MODEL), D_FF),
    }


def reference(x, positions, norm_g, ret_w_in, ret_gn_g, ret_w_out,
              cf_w_pw1, cf_b_pw1, cf_w_dw, cf_b_dw, cf_ln_g, cf_ln_b, cf_w_pw2, cf_b_pw2,
              sc_w_in, sc_w_conv, sc_w_out, mlp_w1, mlp_w2):
    for i in range(DEPTH):
        kind, j = i % N_MIXERS, i // N_MIXERS
        h = rms_norm(x, norm_g[i, 0])
        if kind == 0:
            h = retention_mixer(h, positions, ret_w_in[j], ret_gn_g[j], ret_w_out[j])
        elif kind == 1:
            h = conformer_conv_mixer(h, cf_w_pw1[j], cf_b_pw1[j], cf_w_dw[j], cf_b_dw[j],
                                     cf_ln_g[j], cf_ln_b[j], cf_w_pw2[j], cf_b_pw2[j])
        else:
            h = short_conv_mixer(h, sc_w_in[j], sc_w_conv[j], sc_w_out[j])
        x = x + rms_norm(h, norm_g[i, 1])
        h = squared_relu_mlp(rms_norm(x, norm_g[i, 2]), mlp_w1[i], mlp_w2[i])
        x = x + rms_norm(h, norm_g[i, 3])
    return x
```

```python
import functools
import math

import jax
import jax.numpy as jnp
from jax import lax
from jax.experimental import pallas as pl
from jax.experimental.pallas import tpu as pltpu

F32 = jnp.float32
BF16 = jnp.bfloat16

N_MIXERS = 3
RET_HEADS = 4
ROPE_BASE = 10000.0
RMS_EPS = 1e-6
LN_EPS = 1e-5

LANES = 128
SUBLANES = 8
ROW_TILE = 512
RET_CHUNK = 256
VMEM_LIMIT_BYTES = 56 * 1024 * 1024


def _rms(x, g):
    return x * lax.rsqrt(jnp.mean(x * x, axis=-1, keepdims=True) + RMS_EPS) * g


def _dot(a, b):
    return jnp.dot(a, b, preferred_element_type=F32)


def _const_spec(shape):
    return pl.BlockSpec(shape, lambda *_: (0,) * len(shape), pipeline_mode=pl.Buffered(1))


def _params():
    return pltpu.CompilerParams(
        dimension_semantics=("arbitrary", "arbitrary"), vmem_limit_bytes=VMEM_LIMIT_BYTES)


def _row_spec(tm, width, tiles_per_seq):
    return pl.BlockSpec((tm, width), lambda b, s: (b * tiles_per_seq + s, 0))


def _mlp_kernel(x_ref, g_ref, w1_ref, w2_ref, o_ref, h_s, *, fc, nc):
    x = x_ref[...]
    d = x.shape[1]
    f = w1_ref.shape[1]
    h = _rms(x, g_ref[2:3, :]).astype(BF16)
    for c in range(f // fc):
        a = jnp.maximum(_dot(h, w1_ref[:, c * fc:(c + 1) * fc]), 0.0)
        h_s[:, c * fc:(c + 1) * fc] = (a * a).astype(BF16)
    ssq = jnp.zeros((x.shape[0], 1), F32)
    for j in range(d // nc):
        y = _dot(h_s[...], w2_ref[:, j * nc:(j + 1) * nc])
        ssq = ssq + jnp.sum(y * y, axis=-1, keepdims=True)
        o_ref[:, j * nc:(j + 1) * nc] = y
    scale = lax.rsqrt(ssq * (1.0 / d) + RMS_EPS)
    o_ref[...] = x + o_ref[...] * scale * g_ref[3:4, :]


def _mlp_sublayer(x2, g, w1, w2, *, batch, tm):
    n, d = x2.shape
    f = w1.shape[1]
    tiles = n // batch // tm
    return pl.pallas_call(
        functools.partial(_mlp_kernel, fc=512, nc=256),
        grid=(batch, tiles),
        in_specs=[_row_spec(tm, d, tiles), _const_spec((4, d)), _const_spec((d, f)), _const_spec((f, d))],
        out_specs=_row_spec(tm, d, tiles),
        out_shape=jax.ShapeDtypeStruct((n, d), F32),
        scratch_shapes=[pltpu.VMEM((tm, f), BF16)],
        compiler_params=_params(),
        name="mlp_sublayer",
    )(x2, g, w1, w2)


def _rope_kernel(pos_ref, invf_ref, cos_ref, sin_ref):
    ang = pos_ref[...].astype(F32) * invf_ref[...]
    cos_ref[...] = jnp.cos(ang)
    sin_ref[...] = jnp.sin(ang)


def _rope_tables(positions, half):
    n = positions.size
    tm = 2048
    inv_freq = (ROPE_BASE ** (-jnp.arange(0, 2 * half, 2, dtype=F32) / (2 * half))).reshape(1, half)
    return pl.pallas_call(
        _rope_kernel,
        grid=(n // tm,),
        in_specs=[pl.BlockSpec((tm, 1), lambda i: (i, 0)), pl.BlockSpec((1, half), lambda i: (0, 0))],
        out_specs=[pl.BlockSpec((tm, half), lambda i: (i, 0))] * 2,
        out_shape=[jax.ShapeDtypeStruct((n, half), F32)] * 2,
        name="rope_tables",
    )(positions.reshape(n, 1), inv_freq)


def _ret_kernel(x_ref, cos_ref, sin_ref, g_ref, win_ref, gng_ref, wout_ref, intra_ref, qdec_ref, kdec_ref,
                o_ref, q_s, qd_s, k_s, kd_s, v_s, gate_s, y_s, state_s, *, chunk, chunk_decay, nc):
    heads = len(chunk_decay)
    tm, d = x_ref.shape
    dk = d // heads
    dv = v_s.shape[1] // heads
    half = dk // 2
    n_chunks = tm // chunk

    @pl.when(pl.program_id(1) == 0)
    def _():
        state_s[...] = jnp.zeros_like(state_s)

    x = x_ref[...]
    h = _rms(x, g_ref[0:1, :]).astype(BF16)
    cos = cos_ref[...]
    sin = sin_ref[...]

    def rotary(t):
        t1, t2 = t[:, :half], t[:, half:]
        return t1 * cos - t2 * sin, t2 * cos + t1 * sin

    for hd in range(heads):
        qdec = qdec_ref[hd]
        kdec = kdec_ref[hd]
        q1, q2 = rotary(_dot(h, win_ref[:, hd * dk:(hd + 1) * dk]))
        k1, k2 = rotary(_dot(h, win_ref[:, d + hd * dk:d + (hd + 1) * dk]))
        k1 = k1 * (dk ** -0.5)
        k2 = k2 * (dk ** -0.5)
        lo = slice(hd * dk, hd * dk + half)
        hi = slice(hd * dk + half, (hd + 1) * dk)
        q_s[:, lo] = q1.astype(BF16)
        q_s[:, hi] = q2.astype(BF16)
        k_s[:, lo] = k1.astype(BF16)
        k_s[:, hi] = k2.astype(BF16)
        for c in range(n_chunks):
            rows = slice(c * chunk, (c + 1) * chunk)
            qd_s[rows, lo] = (q1[rows] * qdec).astype(BF16)
            qd_s[rows, hi] = (q2[rows] * qdec).astype(BF16)
            kd_s[rows, lo] = (k1[rows] * kdec).astype(BF16)
            kd_s[rows, hi] = (k2[rows] * kdec).astype(BF16)
    vw = v_s.shape[1]
    for j in range(vw // 512):
        cols = slice(j * 512, (j + 1) * 512)
        v_s[:, cols] = _dot(h, win_ref[:, 2 * d + j * 512:2 * d + (j + 1) * 512]).astype(BF16)
        gate_s[:, cols] = _dot(h, win_ref[:, 2 * d + vw + j * 512:2 * d + vw + (j + 1) * 512])

    for c in range(n_chunks):
        rows = slice(c * chunk, (c + 1) * chunk)
        for hd in range(heads):
            kcols = slice(hd * dk, (hd + 1) * dk)
            vcols = slice(hd * dv, (hd + 1) * dv)
            qc = q_s[rows, kcols]
            kc = k_s[rows, kcols]
            vc = v_s[rows, vcols]
            scores = lax.dot_general(qc, kc, (((1,), (1,)), ((), ())), preferred_element_type=F32)
            p = (scores * intra_ref[hd]).astype(BF16)
            st = state_s[hd]
            o = _dot(p, vc) + _dot(qd_s[rows, kcols], st.astype(BF16))
            state_s[hd] = st * chunk_decay[hd] + lax.dot_general(
                kd_s[rows, kcols], vc, (((0,), (0,)), ((), ())), preferred_element_type=F32)
            mean = jnp.mean(o, axis=-1, keepdims=True)
            oc = o - mean
            var = jnp.mean(oc * oc, axis=-1, keepdims=True)
            yn = oc * lax.rsqrt(var + LN_EPS) * gng_ref[:, vcols]
            gt = gate_s[rows, vcols]
            y_s[rows, vcols] = (gt * jax.nn.sigmoid(gt) * yn).astype(BF16)

    ssq = jnp.zeros((tm, 1), F32)
    for j in range(d // nc):
        y = _dot(y_s[...], wout_ref[:, j * nc:(j + 1) * nc])
        ssq = ssq + jnp.sum(y * y, axis=-1, keepdims=True)
        o_ref[:, j * nc:(j + 1) * nc] = y
    scale = lax.rsqrt(ssq * (1.0 / d) + RMS_EPS)
    o_ref[...] = x + o_ref[...] * scale * g_ref[1:2, :]


def _retention_constants(chunk):
    hidx = jnp.arange(RET_HEADS, dtype=F32)
    log_gamma = jnp.log1p(-jnp.exp2(-5.0 - hidx))
    idx = jnp.arange(chunk, dtype=F32)
    rel = idx[:, None] - idx[None, :]
    intra = jnp.where(rel >= 0, jnp.exp(log_gamma[:, None, None] * jnp.maximum(rel, 0.0)), 0.0)
    q_decay = jnp.exp(log_gamma[:, None] * (idx + 1.0))
    k_decay = jnp.exp(log_gamma[:, None] * (chunk - 1.0 - idx))
    rep = lambda t: jnp.broadcast_to(t[:, :, None], (RET_HEADS, chunk, LANES))
    chunk_decay = tuple(math.exp(math.log1p(-(2.0 ** (-5.0 - hd))) * chunk) for hd in range(RET_HEADS))
    return intra, rep(q_decay), rep(k_decay), chunk_decay


def _retention_sublayer(x2, cos, sin, g, w_in, gn_g, w_out, *, batch, tm):
    n, d = x2.shape
    vw = w_out.shape[0]
    tiles = n // batch // tm
    chunk = RET_CHUNK
    intra, qdec, kdec, chunk_decay = _retention_constants(chunk)
    half = cos.shape[1]
    return pl.pallas_call(
        functools.partial(_ret_kernel, chunk=chunk, chunk_decay=chunk_decay, nc=256),
        grid=(batch, tiles),
        in_specs=[_row_spec(tm, d, tiles), _row_spec(tm, half, tiles), _row_spec(tm, half, tiles),
                  _const_spec((4, d)), _const_spec(w_in.shape), _const_spec((1, vw)), _const_spec(w_out.shape),
                  _const_spec(intra.shape), _const_spec(qdec.shape), _const_spec(kdec.shape)],
        out_specs=_row_spec(tm, d, tiles),
        out_shape=jax.ShapeDtypeStruct((n, d), F32),
        scratch_shapes=[pltpu.VMEM((tm, d), BF16), pltpu.VMEM((tm, d), BF16), pltpu.VMEM((tm, d), BF16),
                        pltpu.VMEM((tm, d), BF16), pltpu.VMEM((tm, vw), BF16), pltpu.VMEM((tm, vw), F32),
                        pltpu.VMEM((tm, vw), BF16), pltpu.VMEM((RET_HEADS, d // RET_HEADS, vw // RET_HEADS), F32)],
        compiler_params=_params(),
        name="retention_sublayer",
    )(x2, cos, sin, g, w_in, gn_g.reshape(1, vw), w_out, intra, qdec, kdec)


def _depthwise_conv(src_ref, first_row, w_ref, bias, dst_ref, *, rows_per_block, lanes_per_block):
    taps = w_ref.shape[0]
    tm, d = dst_ref.shape
    for lb in range(d // lanes_per_block):
        cols = slice(lb * lanes_per_block, (lb + 1) * lanes_per_block)
        for rb in range(tm // rows_per_block):
            r0 = rb * rows_per_block
            acc = None
            for k in range(taps):
                term = w_ref[k:k + 1, cols] * src_ref[first_row + r0 + k:first_row + r0 + k + rows_per_block, cols]
                acc = term if acc is None else acc + term
            if bias is not None:
                acc = acc + bias[:, cols]
            dst_ref[r0:r0 + rows_per_block, cols] = acc


def _cf_kernel(x_ref, g_ref, w1_ref, b1_ref, wdw_ref, bdw_ref, lng_ref, lnb_ref, w2_ref, b2_ref,
               o_ref, glu_s, conv_s, *, halo, nc):
    tm, d = x_ref.shape
    taps = wdw_ref.shape[0]

    @pl.when(pl.program_id(1) == 0)
    def _():
        glu_s[0:halo, :] = jnp.zeros((halo, d), F32)

    x = x_ref[...]
    h = _rms(x, g_ref[0:1, :]).astype(BF16)
    for j in range(d // 512):
        cols = slice(j * 512, (j + 1) * 512)
        a = _dot(h, w1_ref[:, j * 512:(j + 1) * 512]) + b1_ref[:, cols]
        gate = _dot(h, w1_ref[:, d + j * 512:d + (j + 1) * 512]) + b1_ref[:, d + j * 512:d + (j + 1) * 512]
        glu_s[halo:halo + tm, cols] = a * jax.nn.sigmoid(gate)
    _depthwise_conv(glu_s, halo - (taps - 1), wdw_ref, bdw_ref[...], conv_s, rows_per_block=64, lanes_per_block=256)
    glu_s[0:halo, :] = glu_s[tm:tm + halo, :]
    c = conv_s[...]
    mean = jnp.mean(c, axis=-1, keepdims=True)
    cc = c - mean
    var = jnp.mean(cc * cc, axis=-1, keepdims=True)
    hf = cc * lax.rsqrt(var + LN_EPS) * lng_ref[...] + lnb_ref[...]
    hb = (hf * jax.nn.sigmoid(hf)).astype(BF16)
    ssq = jnp.zeros((tm, 1), F32)
    for j in range(d // nc):
        y = _dot(hb, w2_ref[:, j * nc:(j + 1) * nc]) + b2_ref[:, j * nc:(j + 1) * nc]
        ssq = ssq + jnp.sum(y * y, axis=-1, keepdims=True)
        o_ref[:, j * nc:(j + 1) * nc] = y
    scale = lax.rsqrt(ssq * (1.0 / d) + RMS_EPS)
    o_ref[...] = x + o_ref[...] * scale * g_ref[1:2, :]


def _conformer_sublayer(x2, g, w_pw1, b_pw1, w_dw, b_dw, ln_g, ln_b, w_pw2, b_pw2, *, batch, tm):
    n, d = x2.shape
    tiles = n // batch // tm
    taps = w_dw.shape[0]
    halo = -(-(taps - 1) // SUBLANES) * SUBLANES
    row = lambda t: t.reshape(1, -1)
    return pl.pallas_call(
        functools.partial(_cf_kernel, halo=halo, nc=256),
        grid=(batch, tiles),
        in_specs=[_row_spec(tm, d, tiles), _const_spec((4, d)), _const_spec(w_pw1.shape), _const_spec((1, 2 * d)),
                  _const_spec(w_dw.shape), _const_spec((1, d)), _const_spec((1, d)), _const_spec((1, d)),
                  _const_spec(w_pw2.shape), _const_spec((1, d))],
        out_specs=_row_spec(tm, d, tiles),
        out_shape=jax.ShapeDtypeStruct((n, d), F32),
        scratch_shapes=[pltpu.VMEM((tm + halo, d), F32), pltpu.VMEM((tm, d), F32)],
        compiler_params=_params(),
        name="conformer_sublayer",
    )(x2, g, w_pw1, row(b_pw1), w_dw, row(b_dw), row(ln_g), row(ln_b), w_pw2, row(b_pw2))


def _sc_kernel(x_ref, g_ref, win_ref, wconv_ref, wout_ref, o_ref, gb_s, p_s, u_s, *, halo, nc):
    tm, d = x_ref.shape
    taps = wconv_ref.shape[0]

    @pl.when(pl.program_id(1) == 0)
    def _():
        p_s[0:halo, :] = jnp.zeros((halo, d), F32)

    x = x_ref[...]
    h = _rms(x, g_ref[0:1, :]).astype(BF16)
    for j in range(d // 512):
        cols = slice(j * 512, (j + 1) * 512)
        gb_s[:, cols] = _dot(h, win_ref[:, j * 512:(j + 1) * 512])
        gc = _dot(h, win_ref[:, d + j * 512:d + (j + 1) * 512])
        hh = _dot(h, win_ref[:, 2 * d + j * 512:2 * d + (j + 1) * 512])
        p_s[halo:halo + tm, cols] = gc * hh
    _depthwise_conv(p_s, halo - (taps - 1), wconv_ref, None, u_s, rows_per_block=64, lanes_per_block=256)
    p_s[0:halo, :] = p_s[tm:tm + halo, :]
    ub = (gb_s[...] * u_s[...]).astype(BF16)
    ssq = jnp.zeros((tm, 1), F32)
    for j in range(d // nc):
        y = _dot(ub, wout_ref[:, j * nc:(j + 1) * nc])
        ssq = ssq + jnp.sum(y * y, axis=-1, keepdims=True)
        o_ref[:, j * nc:(j + 1) * nc] = y
    scale = lax.rsqrt(ssq * (1.0 / d) + RMS_EPS)
    o_ref[...] = x + o_ref[...] * scale * g_ref[1:2, :]


def _shortconv_sublayer(x2, g, w_in, w_conv, w_out, *, batch, tm):
    n, d = x2.shape
    tiles = n // batch // tm
    halo = SUBLANES
    return pl.pallas_call(
        functools.partial(_sc_kernel, halo=halo, nc=256),
        grid=(batch, tiles),
        in_specs=[_row_spec(tm, d, tiles), _const_spec((4, d)), _const_spec(w_in.shape), _const_spec(w_conv.shape),
                  _const_spec(w_out.shape)],
        out_specs=_row_spec(tm, d, tiles),
        out_shape=jax.ShapeDtypeStruct((n, d), F32),
        scratch_shapes=[pltpu.VMEM((tm, d), F32), pltpu.VMEM((tm + halo, d), F32), pltpu.VMEM((tm, d), F32)],
        compiler_params=_params(),
        name="shortconv_sublayer",
    )(x2, g, w_in, w_conv, w_out)


def kernel(x, positions, norm_g, ret_w_in, ret_gn_g, ret_w_out, cf_w_pw1, cf_b_pw1, cf_w_dw, cf_b_dw, cf_ln_g,
           cf_ln_b, cf_w_pw2, cf_b_pw2, sc_w_in, sc_w_conv, sc_w_out, mlp_w1, mlp_w2):
    batch, seq, d = x.shape
    depth = norm_g.shape[0]
    tm = ROW_TILE
    assert seq % tm == 0 and tm % RET_CHUNK == 0
    x2 = x.reshape(batch * seq, d)
    bf = lambda w: w.astype(BF16)
    cos, sin = _rope_tables(positions, d // RET_HEADS // 2)
    for i in range(depth):
        kind, j = i % N_MIXERS, i // N_MIXERS
        g = norm_g[i]
        if kind == 0:
            x2 = _retention_sublayer(x2, cos, sin, g, bf(ret_w_in[j]), ret_gn_g[j], bf(ret_w_out[j]),
                                     batch=batch, tm=tm)
        elif kind == 1:
            x2 = _conformer_sublayer(x2, g, bf(cf_w_pw1[j]), cf_b_pw1[j], cf_w_dw[j], cf_b_dw[j], cf_ln_g[j],
                                     cf_ln_b[j], bf(cf_w_pw2[j]), cf_b_pw2[j], batch=batch, tm=tm)
        else:
            x2 = _shortconv_sublayer(x2, g, bf(sc_w_in[j]), sc_w_conv[j], bf(sc_w_out[j]), batch=batch, tm=tm)
        x2 = _mlp_sublayer(x2, g, bf(mlp_w1[i]), bf(mlp_w2[i]), batch=batch, tm=tm)
    return x2.reshape(batch, seq, d)
```

```python
import functools
import math

import jax
import jax.numpy as jnp
from jax import lax
from jax.experimental import pallas as pl
from jax.experimental.pallas import tpu as pltpu

F32 = jnp.float32
BF16 = jnp.bfloat16

N_MIXERS = 3
RET_HEADS = 4
ROPE_BASE = 10000.0
RMS_EPS = 1e-6
LN_EPS = 1e-5

LANES = 128
SUBLANES = 8
ROW_TILE = 512
MLP_ROW_TILE = 1024
RET_CHUNK = 256
VMEM_LIMIT_BYTES = 56 * 1024 * 1024


def _rms(x, g):
    return x * lax.rsqrt(jnp.mean(x * x, axis=-1, keepdims=True) + RMS_EPS) * g


def _dot(a, b):
    return jnp.dot(a, b, preferred_element_type=F32)


def _const_spec(shape):
    return pl.BlockSpec(shape, lambda *_: (0,) * len(shape), pipeline_mode=pl.Buffered(1))


def _params():
    return pltpu.CompilerParams(
        dimension_semantics=("arbitrary", "arbitrary"), vmem_limit_bytes=VMEM_LIMIT_BYTES)


def _row_spec(tm, width, tiles_per_seq):
    return pl.BlockSpec((tm, width), lambda b, s: (b * tiles_per_seq + s, 0))


def _mlp_kernel(x_ref, g_ref, w1_ref, w2_ref, o_ref, h_s, *, sub, fc, nc):
    tm, d = x_ref.shape
    f = w1_ref.shape[1]
    for sb in range(tm // sub):
        rows = slice(sb * sub, (sb + 1) * sub)
        hid = h_s.at[sb % h_s.shape[0]]
        x = x_ref[rows, :]
        h = _rms(x, g_ref[2:3, :]).astype(BF16)
        for c in range(f // fc):
            a = jnp.maximum(_dot(h, w1_ref[:, c * fc:(c + 1) * fc]), 0.0)
            hid[:, c * fc:(c + 1) * fc] = (a * a).astype(BF16)
        ssq = jnp.zeros((sub, 1), F32)
        for j in range(d // nc):
            y = _dot(hid[...], w2_ref[:, j * nc:(j + 1) * nc])
            ssq = ssq + jnp.sum(y * y, axis=-1, keepdims=True)
            o_ref[rows, j * nc:(j + 1) * nc] = y
        scale = lax.rsqrt(ssq * (1.0 / d) + RMS_EPS)
        o_ref[rows, :] = x + o_ref[rows, :] * scale * g_ref[3:4, :]


def _mlp_sublayer(x2, g, w1, w2, *, batch, tm, sub=256):
    n, d = x2.shape
    f = w1.shape[1]
    tiles = n // batch // tm
    return pl.pallas_call(
        functools.partial(_mlp_kernel, sub=sub, fc=512, nc=256),
        grid=(batch, tiles),
        in_specs=[_row_spec(tm, d, tiles), _const_spec((4, d)), _const_spec((d, f)), _const_spec((f, d))],
        out_specs=_row_spec(tm, d, tiles),
        out_shape=jax.ShapeDtypeStruct((n, d), F32),
        scratch_shapes=[pltpu.VMEM((2, sub, f), BF16)],
        compiler_params=_params(),
        name="mlp_sublayer",
    )(x2, g, w1, w2)


def _rope_kernel(pos_ref, invf_ref, cos_ref, sin_ref):
    ang = pos_ref[...].astype(F32) * invf_ref[...]
    cos_ref[...] = jnp.cos(ang)
    sin_ref[...] = jnp.sin(ang)


def _rope_tables(positions, half):
    n = positions.size
    tm = 2048
    inv_freq = (ROPE_BASE ** (-jnp.arange(0, 2 * half, 2, dtype=F32) / (2 * half))).reshape(1, half)
    return pl.pallas_call(
        _rope_kernel,
        grid=(n // tm,),
        in_specs=[pl.BlockSpec((tm, 1), lambda i: (i, 0)), pl.BlockSpec((1, half), lambda i: (0, 0))],
        out_specs=[pl.BlockSpec((tm, half), lambda i: (i, 0))] * 2,
        out_shape=[jax.ShapeDtypeStruct((n, half), F32)] * 2,
        name="rope_tables",
    )(positions.reshape(n, 1), inv_freq)


def _ret_kernel(x_ref, cos_ref, sin_ref, g_ref, win_ref, gng_ref, wout_ref, intra_ref, qdec_ref, kdec_ref,
                o_ref, h_s, q_s, qd_s, k_s, kd_s, v_s, gate_s, y_s, state_s, *, chunk_decay, nc):
    heads = len(chunk_decay)
    tm, d = x_ref.shape
    n_slots, chunk, vw = y_s.shape
    dk = d // heads
    dv = vw // heads
    half = dk // 2

    @pl.when(pl.program_id(1) == 0)
    def _():
        state_s[...] = jnp.zeros_like(state_s)

    for c in range(tm // chunk):
        rows = slice(c * chunk, (c + 1) * chunk)
        slot = c % n_slots
        h_s[slot] = _rms(x_ref[rows, :], g_ref[0:1, :]).astype(BF16)
        cos = cos_ref[rows, :]
        sin = sin_ref[rows, :]

        def rotary(t):
            t1, t2 = t[:, :half], t[:, half:]
            return t1 * cos - t2 * sin, t2 * cos + t1 * sin

        for hd in range(heads):
            qdec = qdec_ref[hd]
            kdec = kdec_ref[hd]
            q1, q2 = rotary(_dot(h_s[slot], win_ref[:, hd * dk:(hd + 1) * dk]))
            k1, k2 = rotary(_dot(h_s[slot], win_ref[:, d + hd * dk:d + (hd + 1) * dk]))
            k1 = k1 * (dk ** -0.5)
            k2 = k2 * (dk ** -0.5)
            lo = slice(hd * dk, hd * dk + half)
            hi = slice(hd * dk + half, (hd + 1) * dk)
            q_s[slot, :, lo] = q1.astype(BF16)
            q_s[slot, :, hi] = q2.astype(BF16)
            k_s[slot, :, lo] = k1.astype(BF16)
            k_s[slot, :, hi] = k2.astype(BF16)
            kd_s[slot, :, lo] = (k1 * kdec).astype(BF16)
            kd_s[slot, :, hi] = (k2 * kdec).astype(BF16)
            qd_s[slot, :, lo] = (q1 * qdec).astype(BF16)
            qd_s[slot, :, hi] = (q2 * qdec).astype(BF16)
        for hd in range(heads):
            vcols = slice(hd * dv, (hd + 1) * dv)
            v_s[slot, :, vcols] = _dot(h_s[slot], win_ref[:, 2 * d + hd * dv:2 * d + (hd + 1) * dv]).astype(BF16)
            gate_s[slot, :, vcols] = _dot(h_s[slot], win_ref[:, 2 * d + vw + hd * dv:2 * d + vw + (hd + 1) * dv])

        for hd in range(heads):
            kcols = slice(hd * dk, (hd + 1) * dk)
            vcols = slice(hd * dv, (hd + 1) * dv)
            scores = lax.dot_general(q_s[slot, :, kcols], k_s[slot, :, kcols], (((1,), (1,)), ((), ())),
                                     preferred_element_type=F32)
            p = (scores * intra_ref[hd]).astype(BF16)
            vc = v_s[slot, :, vcols]
            st = state_s[hd]
            o = _dot(p, vc) + _dot(qd_s[slot, :, kcols], st.astype(BF16))
            state_s[hd] = st * chunk_decay[hd] + lax.dot_general(
                kd_s[slot, :, kcols], vc, (((0,), (0,)), ((), ())), preferred_element_type=F32)
            mean = jnp.mean(o, axis=-1, keepdims=True)
            oc = o - mean
            var = jnp.mean(oc * oc, axis=-1, keepdims=True)
            yn = oc * lax.rsqrt(var + LN_EPS) * gng_ref[:, vcols]
            gt = gate_s[slot, :, vcols]
            y_s[slot, :, vcols] = (gt * jax.nn.sigmoid(gt) * yn).astype(BF16)

        ssq = jnp.zeros((chunk, 1), F32)
        for j in range(d // nc):
            y = _dot(y_s[slot], wout_ref[:, j * nc:(j + 1) * nc])
            ssq = ssq + jnp.sum(y * y, axis=-1, keepdims=True)
            o_ref[rows, j * nc:(j + 1) * nc] = y
        scale = lax.rsqrt(ssq * (1.0 / d) + RMS_EPS)
        o_ref[rows, :] = x_ref[rows, :] + o_ref[rows, :] * scale * g_ref[1:2, :]


def _retention_constants(chunk):
    hidx = jnp.arange(RET_HEADS, dtype=F32)
    log_gamma = jnp.log1p(-jnp.exp2(-5.0 - hidx))
    idx = jnp.arange(chunk, dtype=F32)
    rel = idx[:, None] - idx[None, :]
    intra = jnp.where(rel >= 0, jnp.exp(log_gamma[:, None, None] * jnp.maximum(rel, 0.0)), 0.0)
    q_decay = jnp.exp(log_gamma[:, None] * (idx + 1.0))
    k_decay = jnp.exp(log_gamma[:, None] * (chunk - 1.0 - idx))
    rep = lambda t: jnp.broadcast_to(t[:, :, None], (RET_HEADS, chunk, LANES))
    chunk_decay = tuple(math.exp(math.log1p(-(2.0 ** (-5.0 - hd))) * chunk) for hd in range(RET_HEADS))
    return intra, rep(q_decay), rep(k_decay), chunk_decay


def _retention_sublayer(x2, cos, sin, g, w_in, gn_g, w_out, *, batch, tm):
    n, d = x2.shape
    vw = w_out.shape[0]
    dk, dv = d // RET_HEADS, vw // RET_HEADS
    tiles = n // batch // tm
    chunk = RET_CHUNK
    slots = 2
    intra, qdec, kdec, chunk_decay = _retention_constants(chunk)
    half = cos.shape[1]
    return pl.pallas_call(
        functools.partial(_ret_kernel, chunk_decay=chunk_decay, nc=256),
        grid=(batch, tiles),
        in_specs=[_row_spec(tm, d, tiles), _row_spec(tm, half, tiles), _row_spec(tm, half, tiles),
                  _const_spec((4, d)), _const_spec(w_in.shape), _const_spec((1, vw)), _const_spec(w_out.shape),
                  _const_spec(intra.shape), _const_spec(qdec.shape), _const_spec(kdec.shape)],
        out_specs=_row_spec(tm, d, tiles),
        out_shape=jax.ShapeDtypeStruct((n, d), F32),
        scratch_shapes=[pltpu.VMEM((slots, chunk, d), BF16), pltpu.VMEM((slots, chunk, d), BF16),
                        pltpu.VMEM((slots, chunk, d), BF16), pltpu.VMEM((slots, chunk, d), BF16),
                        pltpu.VMEM((slots, chunk, d), BF16), pltpu.VMEM((slots, chunk, vw), BF16),
                        pltpu.VMEM((slots, chunk, vw), F32), pltpu.VMEM((slots, chunk, vw), BF16),
                        pltpu.VMEM((RET_HEADS, dk, dv), F32)],
        compiler_params=_params(),
        name="retention_sublayer",
    )(x2, cos, sin, g, w_in, gn_g.reshape(1, vw), w_out, intra, qdec, kdec)


def _depthwise_conv(src_ref, slab, first_row, w_ref, bias, dst_ref, *, rows_per_block=64):
    taps = w_ref.shape[0]
    tm = dst_ref.shape[0]
    cols = slice(slab * LANES, (slab + 1) * LANES)
    for rb in range(tm // rows_per_block):
        r0 = rb * rows_per_block
        acc = None
        for k in range(taps):
            term = w_ref[k:k + 1, cols] * src_ref[slab, first_row + r0 + k:first_row + r0 + k + rows_per_block, :]
            acc = term if acc is None else acc + term
        if bias is not None:
            acc = acc + bias[:, cols]
        dst_ref[r0:r0 + rows_per_block, cols] = acc


def _store_slabs(dst_ref, first_slab, first_row, value):
    rows, width = value.shape
    for l in range(width // LANES):
        dst_ref[first_slab + l, first_row:first_row + rows, :] = value[:, l * LANES:(l + 1) * LANES]


def _cf_kernel(x_ref, g_ref, w1_ref, b1_ref, wdw_ref, bdw_ref, lng_ref, lnb_ref, w2_ref, b2_ref,
               o_ref, glu_s, conv_s, *, halo, nc):
    tm, d = x_ref.shape
    taps = wdw_ref.shape[0]

    @pl.when(pl.program_id(1) == 0)
    def _():
        glu_s[:, 0:halo, :] = jnp.zeros((d // LANES, halo, LANES), F32)

    x = x_ref[...]
    h = _rms(x, g_ref[0:1, :]).astype(BF16)
    for j in range(d // 512):
        cols = slice(j * 512, (j + 1) * 512)
        a = _dot(h, w1_ref[:, j * 512:(j + 1) * 512]) + b1_ref[:, cols]
        gate = _dot(h, w1_ref[:, d + j * 512:d + (j + 1) * 512]) + b1_ref[:, d + j * 512:d + (j + 1) * 512]
        _store_slabs(glu_s, j * (512 // LANES), halo, a * jax.nn.sigmoid(gate))
        for l in range(512 // LANES):
            _depthwise_conv(glu_s, j * (512 // LANES) + l, halo - (taps - 1), wdw_ref, bdw_ref[...], conv_s)
    glu_s[:, 0:halo, :] = glu_s[:, tm:tm + halo, :]
    c = conv_s[...]
    mean = jnp.mean(c, axis=-1, keepdims=True)
    cc = c - mean
    var = jnp.mean(cc * cc, axis=-1, keepdims=True)
    hf = cc * lax.rsqrt(var + LN_EPS) * lng_ref[...] + lnb_ref[...]
    hb = (hf * jax.nn.sigmoid(hf)).astype(BF16)
    ssq = jnp.zeros((tm, 1), F32)
    for j in range(d // nc):
        y = _dot(hb, w2_ref[:, j * nc:(j + 1) * nc]) + b2_ref[:, j * nc:(j + 1) * nc]
        ssq = ssq + jnp.sum(y * y, axis=-1, keepdims=True)
        o_ref[:, j * nc:(j + 1) * nc] = y
    scale = lax.rsqrt(ssq * (1.0 / d) + RMS_EPS)
    o_ref[...] = x + o_ref[...] * scale * g_ref[1:2, :]


def _conformer_sublayer(x2, g, w_pw1, b_pw1, w_dw, b_dw, ln_g, ln_b, w_pw2, b_pw2, *, batch, tm):
    n, d = x2.shape
    tiles = n // batch // tm
    taps = w_dw.shape[0]
    halo = -(-(taps - 1) // SUBLANES) * SUBLANES
    row = lambda t: t.reshape(1, -1)
    return pl.pallas_call(
        functools.partial(_cf_kernel, halo=halo, nc=256),
        grid=(batch, tiles),
        in_specs=[_row_spec(tm, d, tiles), _const_spec((4, d)), _const_spec(w_pw1.shape), _const_spec((1, 2 * d)),
                  _const_spec(w_dw.shape), _const_spec((1, d)), _const_spec((1, d)), _const_spec((1, d)),
                  _const_spec(w_pw2.shape), _const_spec((1, d))],
        out_specs=_row_spec(tm, d, tiles),
        out_shape=jax.ShapeDtypeStruct((n, d), F32),
        scratch_shapes=[pltpu.VMEM((d // LANES, tm + halo, LANES), F32), pltpu.VMEM((tm, d), F32)],
        compiler_params=_params(),
        name="conformer_sublayer",
    )(x2, g, w_pw1, row(b_pw1), w_dw, row(b_dw), row(ln_g), row(ln_b), w_pw2, row(b_pw2))


def _sc_kernel(x_ref, g_ref, win_ref, wconv_ref, wout_ref, o_ref, gb_s, p_s, u_s, *, halo, nc):
    tm, d = x_ref.shape
    taps = wconv_ref.shape[0]

    @pl.when(pl.program_id(1) == 0)
    def _():
        p_s[:, 0:halo, :] = jnp.zeros((d // LANES, halo, LANES), F32)

    x = x_ref[...]
    h = _rms(x, g_ref[0:1, :]).astype(BF16)
    for j in range(d // 512):
        cols = slice(j * 512, (j + 1) * 512)
        gb_s[:, cols] = _dot(h, win_ref[:, j * 512:(j + 1) * 512])
        gc = _dot(h, win_ref[:, d + j * 512:d + (j + 1) * 512])
        hh = _dot(h, win_ref[:, 2 * d + j * 512:2 * d + (j + 1) * 512])
        _store_slabs(p_s, j * (512 // LANES), halo, gc * hh)
        for l in range(512 // LANES):
            _depthwise_conv(p_s, j * (512 // LANES) + l, halo - (taps - 1), wconv_ref, None, u_s)
    p_s[:, 0:halo, :] = p_s[:, tm:tm + halo, :]
    ub = (gb_s[...] * u_s[...]).astype(BF16)
    ssq = jnp.zeros((tm, 1), F32)
    for j in range(d // nc):
        y = _dot(ub, wout_ref[:, j * nc:(j + 1) * nc])
        ssq = ssq + jnp.sum(y * y, axis=-1, keepdims=True)
        o_ref[:, j * nc:(j + 1) * nc] = y
    scale = lax.rsqrt(ssq * (1.0 / d) + RMS_EPS)
    o_ref[...] = x + o_ref[...] * scale * g_ref[1:2, :]


def _shortconv_sublayer(x2, g, w_in, w_conv, w_out, *, batch, tm):
    n, d = x2.shape
    tiles = n // batch // tm
    halo = SUBLANES
    return pl.pallas_call(
        functools.partial(_sc_kernel, halo=halo, nc=256),
        grid=(batch, tiles),
        in_specs=[_row_spec(tm, d, tiles), _const_spec((4, d)), _const_spec(w_in.shape), _const_spec(w_conv.shape),
                  _const_spec(w_out.shape)],
        out_specs=_row_spec(tm, d, tiles),
        out_shape=jax.ShapeDtypeStruct((n, d), F32),
        scratch_shapes=[pltpu.VMEM((tm, d), F32), pltpu.VMEM((d // LANES, tm + halo, LANES), F32),
                        pltpu.VMEM((tm, d), F32)],
        compiler_params=_params(),
        name="shortconv_sublayer",
    )(x2, g, w_in, w_conv, w_out)


def kernel(x, positions, norm_g, ret_w_in, ret_gn_g, ret_w_out, cf_w_pw1, cf_b_pw1, cf_w_dw, cf_b_dw, cf_ln_g,
           cf_ln_b, cf_w_pw2, cf_b_pw2, sc_w_in, sc_w_conv, sc_w_out, mlp_w1, mlp_w2):
    batch, seq, d = x.shape
    depth = norm_g.shape[0]
    tm = ROW_TILE
    assert seq % tm == 0 and tm % RET_CHUNK == 0
    x2 = x.reshape(batch * seq, d)
    bf = lambda w: w.astype(BF16)
    cos, sin = _rope_tables(positions, d // RET_HEADS // 2)
    for i in range(depth):
        kind, j = i % N_MIXERS, i // N_MIXERS
        g = norm_g[i]
        if kind == 0:
            x2 = _retention_sublayer(x2, cos, sin, g, bf(ret_w_in[j]), ret_gn_g[j], bf(ret_w_out[j]),
                                     batch=batch, tm=tm)
        elif kind == 1:
            x2 = _conformer_sublayer(x2, g, bf(cf_w_pw1[j]), cf_b_pw1[j], cf_w_dw[j], cf_b_dw[j], cf_ln_g[j],
                                     cf_ln_b[j], bf(cf_w_pw2[j]), cf_b_pw2[j], batch=batch, tm=tm)
        else:
            x2 = _shortconv_sublayer(x2, g, bf(sc_w_in[j]), sc_w_conv[j], bf(sc_w_out[j]), batch=batch, tm=tm)
        x2 = _mlp_sublayer(x2, g, bf(mlp_w1[i]), bf(mlp_w2[i]), batch=batch, tm=MLP_ROW_TILE)
    return x2.reshape(batch, seq, d)
```
